```python
import math
import jax, jax.numpy as jnp
from jax import lax
import numpy as np

D_MODEL = 1024
BATCH = 8
SEQ = 4096
DEPTH = 2

HEAD_DIM = 64
N_HEADS_MOBA = 8
N_HEADS_DIL = 8
N_HEADS_EVEN = N_HEADS_MOBA + N_HEADS_DIL
EVEN_WIDTH = N_HEADS_EVEN * HEAD_DIM
MOBA_BLOCK = 256
MOBA_TOPK = 3
MOBA_Q_CHUNK = 32
DIL_PAIRS = ((128, 1), (512, 4), (2048, 16))
N_HEADS_DIFF = 8
DIFF_HEAD_DIM = 64
DIFF_WIDTH = N_HEADS_DIFF * 2 * DIFF_HEAD_DIM
DIFF_Q_BLOCK = 128
DIFF_EPS = 1e-5
D_FF = 2816
CONV_WIDTH = 3
ROPE_THETA = 10000.0
NORM_EPS = 1e-6
N_EVEN = (DEPTH + 1) // 2
N_ODD = DEPTH // 2

kernel_name = 'hybrid_moba_dilated_diffattn_convffn'

f32 = jnp.float32


def rms_norm(x, g, eps=NORM_EPS):
    xf = x.astype(f32)
    y = xf * lax.rsqrt(jnp.mean(xf * xf, axis=-1, keepdims=True) + eps)
    return (y * g.astype(f32)).astype(x.dtype)


def rope(t):
    S, dim = t.shape[-2], t.shape[-1]
    half = dim // 2
    inv = 1.0 / (ROPE_THETA ** (jnp.arange(half, dtype=f32) * 2.0 / dim))
    ang = jnp.arange(S, dtype=f32)[:, None] * inv[None, :]
    cos, sin = jnp.cos(ang), jnp.sin(ang)
    tf = t.astype(f32)
    t1, t2 = tf[..., :half], tf[..., half:]
    return jnp.concatenate([t1 * cos - t2 * sin, t2 * cos + t1 * sin], axis=-1).astype(t.dtype)


def moba_attention(q, k, v):
    B, H, S, hd = q.shape
    nb = -(-S // MOBA_BLOCK)
    Sp = nb * MOBA_BLOCK
    pad = ((0, 0), (0, 0), (0, Sp - S), (0, 0))
    kb = jnp.pad(k, pad).reshape(B, H, nb, MOBA_BLOCK, hd)
    vb = jnp.pad(v, pad).reshape(B, H, nb, MOBA_BLOCK, hd)
    kmean = jnp.mean(kb.astype(f32), axis=3)
    n_sel = min(MOBA_TOPK, nb - 1)
    scale = hd ** -0.5
    n_chunks = S // MOBA_Q_CHUNK

    def chunk(c):
        q0 = c * MOBA_Q_CHUNK
        qc = lax.dynamic_slice_in_dim(q, q0, MOBA_Q_CHUNK, axis=2)
        blk = q0 // MOBA_BLOCK
        qpos = q0 + jnp.arange(MOBA_Q_CHUNK)
        own_k = lax.dynamic_index_in_dim(kb, blk, axis=2, keepdims=False)
        own_v = lax.dynamic_index_in_dim(vb, blk, axis=2, keepdims=False)
        kpos = blk * MOBA_BLOCK + jnp.arange(MOBA_BLOCK)
        s_own = jnp.einsum('bhqd,bhkd->bhqk', qc, own_k).astype(f32) * scale
        s_own = jnp.where(kpos[None, :] <= qpos[:, None], s_own, -jnp.inf)
        if n_sel == 0:
            p = jax.nn.softmax(s_own, axis=-1)
            return jnp.einsum('bhqk,bhkd->bhqd', p.astype(v.dtype), own_v)
        gate = jnp.einsum('bhqd,bhnd->bhqn', qc.astype(f32), kmean)
        gate = jnp.where(jnp.arange(nb) < blk, gate, -jnp.inf)
        _, idx = lax.top_k(gate, n_sel)
        valid = jnp.arange(n_sel) < blk
        take = jax.vmap(jax.vmap(lambda tb, ib: tb[ib]))
        sel_k = take(kb, idx)
        sel_v = take(vb, idx)
        s_sel = jnp.einsum('bhqd,bhqjkd->bhqjk', qc, sel_k).astype(f32) * scale
        s_sel = jnp.where(valid[:, None], s_sel, -jnp.inf)
        s_all = jnp.concatenate(
            [s_own, s_sel.reshape(B, H, MOBA_Q_CHUNK, n_sel * MOBA_BLOCK)], axis=-1)
        p = jax.nn.softmax(s_all, axis=-1).astype(v.dtype)
        p_own = p[..., :MOBA_BLOCK]
        p_sel = p[..., MOBA_BLOCK:].reshape(B, H, MOBA_Q_CHUNK, n_sel, MOBA_BLOCK)
        return (jnp.einsum('bhqk,bhkd->bhqd', p_own, own_v)
                + jnp.einsum('bhqjk,bhqjkd->bhqd', p_sel, sel_v))

    out = lax.map(chunk, jnp.arange(n_chunks))
    return out.transpose(1, 2, 0, 3, 4).reshape(B, H, S, hd)


def dilated_window_attention(q, k, v, window, dilation):
    B, H, S, hd = q.shape
    n = window // dilation
    L = S // dilation
    nblk = -(-L // n)
    Lp = nblk * n

    def to_residue(t):
        t = t.reshape(B, H, L, dilation, hd).transpose(0, 1, 3, 2, 4)
        return jnp.pad(t, ((0, 0), (0, 0), (0, 0), (0, Lp - L), (0, 0)))

    def band(t):
        own = t.reshape(B, H, dilation, nblk, n, hd)
        prev = jnp.pad(t, ((0, 0), (0, 0), (0, 0), (n, 0), (0, 0)))[..., :Lp, :]
        prev = prev.reshape(B, H, dilation, nblk, n, hd)
        return jnp.concatenate([prev, own], axis=-2)

    qb = to_residue(q).reshape(B, H, dilation, nblk, n, hd)
    kw = band(to_residue(k))
    vw = band(to_residue(v))
    s = jnp.einsum('bhrnqd,bhrnkd->bhrnqk', qb, kw).astype(f32) * (hd ** -0.5)
    i = jnp.arange(n)[:, None]
    j = jnp.arange(2 * n)[None, :]
    blk = jnp.arange(nblk)[:, None, None]
    dist = n + i - j
    kidx = (blk - 1) * n + j
    mask = (dist >= 0) & (dist <= n) & (kidx >= 0)
    s = jnp.where(mask, s, -jnp.inf)
    m = jnp.max(s, axis=-1, keepdims=True)
    p = jnp.exp(s - m)
    den = jnp.sum(p, axis=-1, keepdims=True)
    o = jnp.einsum('bhrnqk,bhrnkd->bhrnqd', (p / den).astype(v.dtype), vw)
    lse = (m + jnp.log(den))[..., 0]
    o = o.reshape(B, H, dilation, Lp, hd)[:, :, :, :L, :]
    o = o.transpose(0, 1, 3, 2, 4).reshape(B, H, S, hd)
    lse = lse.reshape(B, H, dilation, Lp)[:, :, :, :L].transpose(0, 1, 3, 2).reshape(B, H, S)
    return o, lse


def dilated_mixture(q, k, v):
    results = [dilated_window_attention(q, k, v, w, r) for (w, r) in DIL_PAIRS]
    outs = jnp.stack([o for (o, _) in results]).astype(f32)
    lses = jnp.stack([l for (_, l) in results])
    wts = jax.nn.softmax(lses, axis=0)
    return jnp.einsum('gbhs,gbhsd->bhsd', wts, outs).astype(q.dtype)


def even_mixer(h, w_in, w_out):
    B, S, _ = h.shape
    qkv = (h @ w_in).reshape(B, S, 3, N_HEADS_EVEN, HEAD_DIM).transpose(2, 0, 3, 1, 4)
    q, k, v = rope(qkv[0]), rope(qkv[1]), qkv[2]
    na = N_HEADS_MOBA
    a = moba_attention(q[:, :na], k[:, :na], v[:, :na])
    b = dilated_mixture(q[:, na:], k[:, na:], v[:, na:])
    o = jnp.concatenate([a, b], axis=1).transpose(0, 2, 1, 3).reshape(B, S, EVEN_WIDTH)
    return o @ w_out


def diff_attention(q, k, v, lam):
    B, H, _, S, d = q.shape
    nq = S // DIFF_Q_BLOCK
    scale = d ** -0.5
    qb = q.reshape(B, H, 2, nq, DIFF_Q_BLOCK, d).transpose(3, 0, 1, 2, 4, 5)
    kpos = jnp.arange(S)

    def block(args):
        qblk, bi = args
        qpos = bi * DIFF_Q_BLOCK + jnp.arange(DIFF_Q_BLOCK)
        s = jnp.einsum('bhcqd,bhckd->bhcqk', qblk, k).astype(f32) * scale
        s = jnp.where(kpos[None, :] <= qpos[:, None], s, -jnp.inf)
        p = jax.nn.softmax(s, axis=-1)
        a = p[:, :, 0] - lam * p[:, :, 1]
        return jnp.einsum('bhqk,bhkd->bhqd', a.astype(v.dtype), v)

    out = lax.map(block, (qb, jnp.arange(nq)))
    return out.transpose(1, 2, 0, 3, 4).reshape(B, H, S, 2 * d)


def diff_mixer(h, w_qkv, lq1, lk1, lq2, lk2, subln, w_out, lambda_init):
    B, S, _ = h.shape
    H, d = N_HEADS_DIFF, DIFF_HEAD_DIM
    qkv = h @ w_qkv
    q = qkv[..., :2 * H * d].reshape(B, S, H, 2, d).transpose(0, 2, 3, 1, 4)
    k = qkv[..., 2 * H * d:4 * H * d].reshape(B, S, H, 2, d).transpose(0, 2, 3, 1, 4)
    v = qkv[..., 4 * H * d:].reshape(B, S, H, 2 * d).transpose(0, 2, 1, 3)
    q, k = rope(q), rope(k)
    lam = (jnp.exp(jnp.sum(lq1.astype(f32) * lk1.astype(f32)))
           - jnp.exp(jnp.sum(lq2.astype(f32) * lk2.astype(f32))) + lambda_init)
    o = diff_attention(q, k, v, lam)
    o = rms_norm(o, subln, DIFF_EPS) * (1.0 - lambda_init)
    return o.transpose(0, 2, 1, 3).reshape(B, S, DIFF_WIDTH) @ w_out


def causal_depthwise_conv(u, w, b):
    C = u.shape[-1]
    y = lax.conv_general_dilated(
        u, w[:, None, :].astype(u.dtype), window_strides=(1,),
        padding=((CONV_WIDTH - 1, 0),), dimension_numbers=('NWC', 'WIO', 'NWC'),
        feature_group_count=C)
    return y + b


def conv_ffn(h, w_up, conv_w, conv_b, w_down):
    u = causal_depthwise_conv(h @ w_up, conv_w, conv_b)
    gate, val = u[..., :D_FF], u[..., D_FF:]
    return (jax.nn.silu(gate) * val) @ w_down


def setup_inputs(seed: int = 0) -> dict:
    key = jax.random.key(seed)
    ks = jax.random.split(key, 18)

    def nrm(kk, shape, scale):
        return jax.random.normal(kk, shape, f32) * scale

    d = DIFF_HEAD_DIM
    return {
        'x': nrm(ks[0], (BATCH, SEQ, D_MODEL), 1.0),
        'even_norm': 1.0 + nrm(ks[1], (N_EVEN, D_MODEL), 0.1),
        'even_w_in': nrm(ks[2], (N_EVEN, D_MODEL, 3 * EVEN_WIDTH), D_MODEL ** -0.5),
        'even_w_out': nrm(ks[3], (N_EVEN, EVEN_WIDTH, D_MODEL), EVEN_WIDTH ** -0.5),
        'odd_norm': 1.0 + nrm(ks[4], (N_ODD, D_MODEL), 0.1),
        'odd_w_qkv': nrm(ks[5], (N_ODD, D_MODEL, 3 * DIFF_WIDTH), D_MODEL ** -0.5),
        'odd_lambda_q1': nrm(ks[6], (N_ODD, d), 0.1),
        'odd_lambda_k1': nrm(ks[7], (N_ODD, d), 0.1),
        'odd_lambda_q2': nrm(ks[8], (N_ODD, d), 0.1),
        'odd_lambda_k2': nrm(ks[9], (N_ODD, d), 0.1),
        'odd_subln': 1.0 + nrm(ks[10], (N_ODD, 2 * d), 0.1),
        'odd_w_out': nrm(ks[11], (N_ODD, DIFF_WIDTH, D_MODEL), DIFF_WIDTH ** -0.5),
        'ffn_norm': 1.0 + nrm(ks[12], (DEPTH, D_MODEL), 0.1),
        'ffn_w_up': nrm(ks[13], (DEPTH, D_MODEL, 2 * D_FF), D_MODEL ** -0.5),
        'ffn_conv_w': nrm(ks[14], (DEPTH, CONV_WIDTH, 2 * D_FF), CONV_WIDTH ** -0.5),
        'ffn_conv_b': nrm(ks[15], (DEPTH, 2 * D_FF), 0.02),
        'ffn_w_down': nrm(ks[16], (DEPTH, D_FF, D_MODEL), D_FF ** -0.5),
        'final_norm': 1.0 + nrm(ks[17], (D_MODEL,), 0.1),
    }


def reference(x, even_norm, even_w_in, even_w_out, odd_norm, odd_w_qkv,
              odd_lambda_q1, odd_lambda_k1, odd_lambda_q2, odd_lambda_k2, odd_subln,
              odd_w_out, ffn_norm, ffn_w_up, ffn_conv_w, ffn_conv_b, ffn_w_down,
              final_norm):
    for layer in range(DEPTH):
        i = layer // 2
        if layer % 2 == 0:
            x = x + even_mixer(rms_norm(x, even_norm[i]), even_w_in[i], even_w_out[i])
        else:
            lambda_init = 0.8 - 0.6 * math.exp(-0.3 * layer)
            x = x + diff_mixer(rms_norm(x, odd_norm[i]), odd_w_qkv[i],
                               odd_lambda_q1[i], odd_lambda_k1[i],
                               odd_lambda_q2[i], odd_lambda_k2[i],
                               odd_subln[i], odd_w_out[i], lambda_init)
        x = x + conv_ffn(rms_norm(x, ffn_norm[layer]), ffn_w_up[layer],
                         ffn_conv_w[layer], ffn_conv_b[layer], ffn_w_down[layer])
    return rms_norm(x, final_norm)
```

```python
import functools
import math

import numpy as np
import jax
import jax.numpy as jnp
from jax import lax
from jax.experimental import pallas as pl
from jax.experimental.pallas import tpu as pltpu

f32 = jnp.float32
bf16 = jnp.bfloat16

HEAD_DIM = 64
HALF = HEAD_DIM // 2
MOBA_BLOCK = 256
MOBA_TOPK = 3
DIL_PAIRS = ((128, 1), (512, 4), (2048, 16))
DIFF_EPS = 1e-5
NORM_EPS = 1e-6
ROPE_THETA = 10000.0
D_FF = 2816
CONV_WIDTH = 3

LANES = 128
ATT_TILE = 256
ROW_TILE = 512
FF_CHUNK = 256
SUBLANES = 8
NEG = -1e30

assert ATT_TILE == MOBA_BLOCK


def _cparams(vmem_mb, n_axes):
    return pltpu.CompilerParams(
        dimension_semantics=("arbitrary",) * n_axes,
        vmem_limit_bytes=vmem_mb * 1024 * 1024)


def _pair_perm(width):
    idx = np.arange(width)
    p, n = idx // LANES, idx % LANES
    hi, e, d = n // HEAD_DIM, (n % HEAD_DIM) // HALF, n % HALF
    return p * LANES + e * HEAD_DIM + hi * HALF + d


def _prep_qkv_weight(w):
    width = w.shape[1] // 3
    perm = _pair_perm(width)
    cols = np.concatenate([perm, width + perm, 2 * width + np.arange(width)])
    return jnp.take(w, jnp.asarray(cols, dtype=jnp.int32), axis=1).astype(bf16)


def _rope_tables(seq):
    inv = 1.0 / (ROPE_THETA ** (jnp.arange(HALF, dtype=f32) * 2.0 / HEAD_DIM))
    ang = jnp.arange(seq, dtype=f32)[:, None] * inv[None, :]
    cos, sin = jnp.cos(ang), jnp.sin(ang)
    cos_t = jnp.concatenate([cos, cos, cos, cos], axis=1)
    sin_t = jnp.concatenate([-sin, -sin, sin, sin], axis=1)
    return cos_t, sin_t


def _dilated_bias_tables():
    max_w = max(w for w, _ in DIL_PAIRS)
    n_back = max_w // ATT_TILE
    r = np.arange(ATT_TILE)[:, None]
    c = np.arange(ATT_TILE)[None, :]
    tabs = []
    for back in range(n_back + 1):
        delta = back * ATT_TILE + c - r
        cnt = np.zeros_like(delta)
        for w, dil in DIL_PAIRS:
            cnt += ((delta >= 0) & (delta <= w) & (delta % dil == 0)).astype(cnt.dtype)
        with np.errstate(divide="ignore"):
            tabs.append(np.where(cnt > 0, np.log(np.maximum(cnt, 1).astype(np.float64)), NEG))
    return jnp.asarray(np.stack(tabs), dtype=f32)


def _rms(x, g, eps):
    ms = jnp.mean(x * x, axis=-1, keepdims=True)
    return x * lax.rsqrt(ms + eps) * g


def _norm_qkv_body(x_ref, g_ref, w_ref, cq_ref, sq_ref, ck_ref, sk_ref, q_ref, k_ref, vt_ref, *,
                   width):
    xn = _rms(x_ref[...], g_ref[...], NORM_EPS).astype(bf16)
    tm = x_ref.shape[0]

    def rope_store(part, cos_ref, sin_ref, dst_ref):
        y = jnp.dot(xn, w_ref[:, part * width:(part + 1) * width], preferred_element_type=f32)
        for p in range(width // LANES):
            blk = y[:, p * LANES:(p + 1) * LANES]
            rot = pltpu.roll(blk, HEAD_DIM, 1)
            dst_ref[:, p * LANES:(p + 1) * LANES] = (
                blk * cos_ref[...] + rot * sin_ref[...]).astype(bf16)

    rope_store(0, cq_ref, sq_ref, q_ref)
    rope_store(1, ck_ref, sk_ref, k_ref)
    yv = jnp.dot(xn, w_ref[:, 2 * width:3 * width], preferred_element_type=f32)
    for t in range(tm // ATT_TILE):
        vt_ref[t] = yv[t * ATT_TILE:(t + 1) * ATT_TILE, :].T.astype(bf16)


def _norm_qkv(x2d, g, w, tables, seq):
    n, d = x2d.shape
    width = w.shape[1] // 3
    tm = ROW_TILE
    tiles_per_seq = seq // tm
    cq, sq, ck, sk = tables
    tab_spec = pl.BlockSpec((tm, LANES), lambda i: (i % tiles_per_seq, 0))
    return pl.pallas_call(
        functools.partial(_norm_qkv_body, width=width),
        grid=(n // tm,),
        in_specs=[
            pl.BlockSpec((tm, d), lambda i: (i, 0)),
            pl.BlockSpec((1, d), lambda i: (0, 0)),
            pl.BlockSpec((d, 3 * width), lambda i: (0, 0)),
            tab_spec, tab_spec, tab_spec, tab_spec,
        ],
        out_specs=[
            pl.BlockSpec((tm, width), lambda i: (i, 0)),
            pl.BlockSpec((tm, width), lambda i: (i, 0)),
            pl.BlockSpec((tm // ATT_TILE, width, ATT_TILE), lambda i: (i, 0, 0)),
        ],
        out_shape=[
            jax.ShapeDtypeStruct((n, width), bf16),
            jax.ShapeDtypeStruct((n, width), bf16),
            jax.ShapeDtypeStruct((n // ATT_TILE, width, ATT_TILE), bf16),
        ],
        compiler_params=_cparams(48, 1),
        name="norm_qkv_rope",
    )(x2d, g.reshape(1, d), w, cq, sq, ck, sk)


def _attn_body(*refs, mode, lambda_init):
    if mode == "moba":
        q_ref, k_ref, vt_ref, o_ref, m_ref, l_ref, acc_ref, kmean_ref, selb_ref = refs
    elif mode == "dil":
        q_ref, k_ref, vt_ref, bias_ref, o_ref, m_ref, l_ref, acc_ref = refs
    else:
        q_ref, k_ref, vt_ref, lam_ref, subln_ref, o_ref, m_ref, l_ref, acc_ref = refs
    tq = tk = ATT_TILE
    i = pl.program_id(2)
    n_tiles = k_ref.shape[0] // tk
    dv = acc_ref.shape[1]

    lane = lax.broadcasted_iota(jnp.int32, (1, LANES), 1)
    elem = (lane % HEAD_DIM) // HALF
    qf = q_ref[...].astype(f32)
    qs = [jnp.where(elem == e, qf, 0.0).astype(bf16) for e in (0, 1)]

    def scores(kt, e):
        return lax.dot_general(kt, qs[e], (((1,), (1,)), ((), ())), preferred_element_type=f32)

    def v_rows(vt, e):
        return vt if dv == LANES else vt[e * dv:(e + 1) * dv, :]

    def update(e, s, vt, first):
        mx = jnp.max(s, axis=0, keepdims=True)
        if first:
            m_new = mx
            p = jnp.exp(s - m_new)
            l_new = jnp.sum(p, axis=0, keepdims=True)
            acc_new = jnp.dot(v_rows(vt, e), p.astype(bf16), preferred_element_type=f32)
        else:
            m_old = m_ref[e]
            m_new = jnp.maximum(m_old, mx)
            alpha = jnp.exp(m_old - m_new)
            p = jnp.exp(s - m_new)
            l_new = alpha * l_ref[e] + jnp.sum(p, axis=0, keepdims=True)
            acc_new = alpha * acc_ref[e] + jnp.dot(
                v_rows(vt, e), p.astype(bf16), preferred_element_type=f32)
        m_ref[e] = m_new
        l_ref[e] = l_new
        acc_ref[e] = acc_new

    if mode == "moba":
        @pl.when(i == 0)
        def _():
            for jb in range(n_tiles):
                blk = k_ref[jb * tk:(jb + 1) * tk, :].astype(f32)
                kmean_ref[jb:jb + 1, :] = jnp.sum(blk, axis=0, keepdims=True) * (1.0 / tk)

        km = kmean_ref[...]
        km_hi = km.astype(bf16)
        r1 = km - km_hi.astype(f32)
        km_mid = r1.astype(bf16)
        km_lo = (r1 - km_mid.astype(f32)).astype(bf16)
        rows = lax.broadcasted_iota(jnp.int32, (n_tiles, tq), 0)
        for e in (0, 1):
            gate = (scores(km_hi, e) + scores(km_mid, e)) + scores(km_lo, e)
            avail = rows < i
            sel = jnp.zeros((n_tiles, tq), dtype=jnp.bool_)
            for _ in range(MOBA_TOPK):
                gm = jnp.where(avail, gate, -jnp.inf)
                mx = jnp.max(gm, axis=0, keepdims=True)
                cand = jnp.where(avail & (gm == mx), rows, n_tiles)
                pick = rows == jnp.min(cand, axis=0, keepdims=True)
                sel = sel | pick
                avail = avail & jnp.logical_not(pick)
            selb_ref[e] = jnp.where(sel, 0.0, NEG)

    r_k = lax.broadcasted_iota(jnp.int32, (tk, tq), 0)
    c_q = lax.broadcasted_iota(jnp.int32, (tk, tq), 1)
    kt = k_ref[pl.ds(pl.multiple_of(i * tk, tk), tk), :]
    vt = vt_ref[i]
    if mode == "dil":
        diag_bias = bias_ref[0]
    else:
        diag_bias = jnp.where(r_k <= c_q, 0.0, NEG)
    for e in (0, 1):
        update(e, scores(kt, e) + diag_bias, vt, first=True)

    def body(j, carry):
        kt = k_ref[pl.ds(pl.multiple_of(j * tk, tk), tk), :]
        vt = vt_ref[j]
        for e in (0, 1):
            s = scores(kt, e)
            if mode == "moba":
                s = s + selb_ref[e, pl.ds(j, 1), :]
            elif mode == "dil":
                s = s + bias_ref[i - j]
            update(e, s, vt, first=False)
        return carry

    if mode == "dil":
        lo = jnp.maximum(i - (bias_ref.shape[0] - 1), 0)
    else:
        lo = 0
    lax.fori_loop(lo, i, body, 0)

    outs = [acc_ref[e] * (1.0 / l_ref[e]) for e in (0, 1)]
    if mode == "diff":
        lv = lam_ref[...]
        lam = (jnp.exp(jnp.sum(lv[0:1] * lv[1:2], axis=-1, keepdims=True))
               - jnp.exp(jnp.sum(lv[2:3] * lv[3:4], axis=-1, keepdims=True)) + lambda_init)
        o_t = outs[0] - lam * outs[1]
        ms = jnp.mean(o_t * o_t, axis=0, keepdims=True)
        o_t = (o_t * lax.rsqrt(ms + DIFF_EPS) * subln_ref[...]) * (1.0 - lambda_init)
    else:
        o_t = jnp.concatenate(outs, axis=0)
    o_ref[...] = o_t.T.astype(bf16)


def _attention(q, k, vt, *, mode, pair_off, n_pairs, batch, seq, extra=(), lambda_init=0.0):
    n, width = q.shape
    tq = ATT_TILE
    n_tiles = seq // tq
    dv = LANES if mode == "diff" else HEAD_DIM
    in_specs = [
        pl.BlockSpec((tq, LANES), lambda b, p, i: (b * n_tiles + i, pair_off + p)),
        pl.BlockSpec((seq, LANES), lambda b, p, i: (b, pair_off + p)),
        pl.BlockSpec((n_tiles, LANES, tq), lambda b, p, i: (b, pair_off + p, 0)),
    ]
    scratch = [
        pltpu.VMEM((2, 1, tq), f32),
        pltpu.VMEM((2, 1, tq), f32),
        pltpu.VMEM((2, dv, tq), f32),
    ]
    if mode == "moba":
        scratch += [pltpu.VMEM((n_tiles, LANES), f32), pltpu.VMEM((2, n_tiles, tq), f32)]
    elif mode == "dil":
        (bias,) = extra
        in_specs.append(pl.BlockSpec(bias.shape, lambda b, p, i: (0, 0, 0)))
    else:
        lam_vecs, subln_b = extra
        in_specs.append(pl.BlockSpec(lam_vecs.shape, lambda b, p, i: (0, 0)))
        in_specs.append(pl.BlockSpec(subln_b.shape, lambda b, p, i: (0, 0)))
    return pl.pallas_call(
        functools.partial(_attn_body, mode=mode, lambda_init=lambda_init),
        grid=(batch, n_pairs, n_tiles),
        in_specs=in_specs,
        out_specs=pl.BlockSpec((tq, LANES), lambda b, p, i: (b * n_tiles + i, p)),
        out_shape=jax.ShapeDtypeStruct((n, n_pairs * LANES), bf16),
        scratch_shapes=scratch,
        compiler_params=_cparams(32, 3),
        name="attn_" + mode,
    )(q, k, vt, *extra)


def _mlp_body(*refs, n_o, final, tiles_per_seq):
    x_ref = refs[0]
    o_refs = refs[1:1 + n_o]
    wo_refs = refs[1 + n_o:1 + 2 * n_o]
    g_ref, wup_ref, cw_ref, cb_ref, wdn_ref = refs[1 + 2 * n_o:6 + 2 * n_o]
    rest = refs[6 + 2 * n_o:]
    if final:
        fg_ref, out_ref, ubuf_ref, abuf_ref, carry_ref = rest
    else:
        out_ref, ubuf_ref, abuf_ref, carry_ref = rest
    tm = x_ref.shape[0]
    i = pl.program_id(0)

    x1 = x_ref[...]
    for o_ref, wo_ref in zip(o_refs, wo_refs):
        x1 = x1 + jnp.dot(o_ref[...], wo_ref[...], preferred_element_type=f32)
    out_ref[...] = x1
    xn = _rms(x1, g_ref[...], NORM_EPS).astype(bf16)

    @pl.when(i % tiles_per_seq == 0)
    def _():
        carry_ref[...] = jnp.zeros_like(carry_ref)

    def conv(slot, col0):
        cols = slice(col0, col0 + FF_CHUNK)
        u = jnp.dot(xn, wup_ref[:, cols], preferred_element_type=f32)
        ubuf_ref[slot, 0:SUBLANES, :] = carry_ref[:, cols]
        ubuf_ref[slot, SUBLANES:SUBLANES + tm, :] = u
        carry_ref[:, cols] = u[tm - SUBLANES:tm, :]
        u1 = ubuf_ref[slot, pl.ds(SUBLANES - 1, tm), :]
        u2 = ubuf_ref[slot, pl.ds(SUBLANES - 2, tm), :]
        w = cw_ref[:, cols]
        return (u * w[2:3] + u1 * w[1:2]) + (u2 * w[0:1] + cb_ref[:, cols])

    for c in range(D_FF // FF_CHUNK):
        slot = 2 * (c % 2)
        gate = conv(slot, c * FF_CHUNK)
        val = conv(slot + 1, D_FF + c * FF_CHUNK)
        act = gate * (1.0 / (1.0 + jnp.exp(-gate))) * val
        abuf_ref[:, c * FF_CHUNK:(c + 1) * FF_CHUNK] = act.astype(bf16)

    x2 = out_ref[...] + jnp.dot(abuf_ref[...], wdn_ref[...], preferred_element_type=f32)
    if final:
        x2 = _rms(x2, fg_ref[...], NORM_EPS)
    out_ref[...] = x2


def _resident(shape):
    return pl.BlockSpec(shape, lambda i: (0,) * len(shape), pipeline_mode=pl.Buffered(1))


def _mlp(x2d, o_list, wo_list, g, w_up, conv_w, conv_b, w_down, final_g, seq):
    n, d = x2d.shape
    tm = ROW_TILE
    n_o = len(o_list)
    final = final_g is not None
    in_specs = [pl.BlockSpec((tm, d), lambda i: (i, 0))]
    in_specs += [pl.BlockSpec((tm, o.shape[1]), lambda i: (i, 0)) for o in o_list]
    in_specs += [_resident(w.shape) for w in wo_list]
    in_specs += [_resident((1, d)), _resident(w_up.shape), _resident(conv_w.shape),
                 _resident((1, 2 * D_FF)), _resident(w_down.shape)]
    args = [x2d, *o_list, *wo_list, g.reshape(1, d), w_up, conv_w, conv_b.reshape(1, 2 * D_FF),
            w_down]
    if final:
        in_specs.append(_resident((1, d)))
        args.append(final_g.reshape(1, d))
    return pl.pallas_call(
        functools.partial(_mlp_body, n_o=n_o, final=final, tiles_per_seq=seq // tm),
        grid=(n // tm,),
        in_specs=in_specs,
        out_specs=pl.BlockSpec((tm, d), lambda i: (i, 0)),
        out_shape=jax.ShapeDtypeStruct((n, d), f32),
        scratch_shapes=[
            pltpu.VMEM((4, tm + SUBLANES, FF_CHUNK), f32),
            pltpu.VMEM((tm, D_FF), bf16),
            pltpu.VMEM((SUBLANES, 2 * D_FF), f32),
        ],
        compiler_params=_cparams(56, 1),
        name="outproj_convmlp",
    )(*args)


def kernel(x, even_norm, even_w_in, even_w_out, odd_norm, odd_w_qkv, odd_lambda_q1, odd_lambda_k1,
           odd_lambda_q2, odd_lambda_k2, odd_subln, odd_w_out, ffn_norm, ffn_w_up, ffn_conv_w,
           ffn_conv_b, ffn_w_down, final_norm):
    batch, seq, d = x.shape
    depth = ffn_norm.shape[0]
    assert seq % ROW_TILE == 0 and ROW_TILE % ATT_TILE == 0
    x2d = x.reshape(batch * seq, d)

    cos_t, sin_t = _rope_tables(seq)
    scale = HEAD_DIM ** -0.5
    tables = (cos_t * scale, sin_t * scale, cos_t, sin_t)
    dil_bias = _dilated_bias_tables()

    for layer in range(depth):
        li = layer // 2
        if layer % 2 == 0:
            w_in = _prep_qkv_weight(even_w_in[li])
            q, k, vt = _norm_qkv(x2d, even_norm[li], w_in, tables, seq)
            n_pairs = q.shape[1] // LANES
            n_moba = n_pairs // 2
            common = dict(batch=batch, seq=seq)
            o_a = _attention(q, k, vt, mode="moba", pair_off=0, n_pairs=n_moba, **common)
            o_b = _attention(q, k, vt, mode="dil", pair_off=n_moba, n_pairs=n_pairs - n_moba,
                             extra=(dil_bias,), **common)
            w_out = even_w_out[li].astype(bf16)
            split = n_moba * LANES
            o_list, wo_list = [o_a, o_b], [w_out[:split], w_out[split:]]
        else:
            lambda_init = 0.8 - 0.6 * math.exp(-0.3 * layer)
            w_qkv = _prep_qkv_weight(odd_w_qkv[li])
            q, k, vt = _norm_qkv(x2d, odd_norm[li], w_qkv, tables, seq)
            lam_vecs = jnp.stack([odd_lambda_q1[li], odd_lambda_k1[li],
                                  odd_lambda_q2[li], odd_lambda_k2[li]]).astype(f32)
            subln_b = jnp.broadcast_to(odd_subln[li].astype(f32)[:, None], (LANES, ATT_TILE))
            o = _attention(q, k, vt, mode="diff", pair_off=0, n_pairs=q.shape[1] // LANES,
                           batch=batch, seq=seq, extra=(lam_vecs, subln_b),
                           lambda_init=lambda_init)
            o_list, wo_list = [o], [odd_w_out[li].astype(bf16)]
        final_g = final_norm if layer == depth - 1 else None
        x2d = _mlp(x2d, o_list, wo_list, ffn_norm[layer], ffn_w_up[layer].astype(bf16),
                   ffn_conv_w[layer], ffn_conv_b[layer], ffn_w_down[layer].astype(bf16),
                   final_g, seq)
    return x2d.reshape(batch, seq, d)
```

```python
import functools
import math

import numpy as np
import jax
import jax.numpy as jnp
from jax import lax
from jax.experimental import pallas as pl
from jax.experimental.pallas import tpu as pltpu

f32 = jnp.float32
bf16 = jnp.bfloat16

HEAD_DIM = 64
HALF = HEAD_DIM // 2
MOBA_BLOCK = 256
MOBA_TOPK = 3
DIL_PAIRS = ((128, 1), (512, 4), (2048, 16))
DIFF_EPS = 1e-5
NORM_EPS = 1e-6
ROPE_THETA = 10000.0
D_FF = 2816
CONV_WIDTH = 3

LANES = 128
ATT_TILE = 256
ROW_TILE = 512
FF_CHUNK = 256
KV_SUB = 2
LOG2E = math.log2(math.e)
SUBLANES = 8
NEG = -1e30

assert ATT_TILE == MOBA_BLOCK


def _cparams(vmem_mb, n_axes):
    return pltpu.CompilerParams(
        dimension_semantics=("arbitrary",) * n_axes,
        vmem_limit_bytes=vmem_mb * 1024 * 1024)


def _pair_perm(width):
    idx = np.arange(width)
    p, n = idx // LANES, idx % LANES
    hi, e, d = n // HEAD_DIM, (n % HEAD_DIM) // HALF, n % HALF
    return p * LANES + e * HEAD_DIM + hi * HALF + d


def _prep_qkv_weight(w):
    width = w.shape[1] // 3
    perm = _pair_perm(width)
    cols = np.concatenate([perm, width + perm, 2 * width + np.arange(width)])
    return jnp.take(w, jnp.asarray(cols, dtype=jnp.int32), axis=1).astype(bf16)


def _rope_tables(seq):
    inv = 1.0 / (ROPE_THETA ** (jnp.arange(HALF, dtype=f32) * 2.0 / HEAD_DIM))
    ang = jnp.arange(seq, dtype=f32)[:, None] * inv[None, :]
    cos, sin = jnp.cos(ang), jnp.sin(ang)
    cos_t = jnp.concatenate([cos, cos, cos, cos], axis=1)
    sin_t = jnp.concatenate([-sin, -sin, sin, sin], axis=1)
    return cos_t, sin_t


def _dilated_bias_tables(n_sub):
    max_w = max(w for w, _ in DIL_PAIRS)
    n_back = max_w // ATT_TILE
    r = np.arange(ATT_TILE)[:, None]
    c = np.arange(ATT_TILE)[None, :]
    tabs = []
    for back in range(-(n_sub - 1), n_back + n_sub):
        delta = back * ATT_TILE + c - r
        cnt = np.zeros_like(delta)
        for w, dil in DIL_PAIRS:
            cnt += ((delta >= 0) & (delta <= w) & (delta % dil == 0)).astype(cnt.dtype)
        with np.errstate(divide="ignore"):
            tabs.append(np.where(cnt > 0, np.log2(np.maximum(cnt, 1).astype(np.float64)), NEG))
    return jnp.asarray(np.stack(tabs), dtype=f32)


def _rms(x, g, eps):
    ms = jnp.mean(x * x, axis=-1, keepdims=True)
    return x * lax.rsqrt(ms + eps) * g


def _norm_qkv_body(x_ref, g_ref, w_ref, cq_ref, sq_ref, ck_ref, sk_ref, q_ref, k_ref, vt_ref, *,
                   width):
    xn = _rms(x_ref[...], g_ref[...], NORM_EPS).astype(bf16)
    tm = x_ref.shape[0]

    def rope_store(part, cos_ref, sin_ref, dst_ref):
        y = jnp.dot(xn, w_ref[:, part * width:(part + 1) * width], preferred_element_type=f32)
        for p in range(width // LANES):
            blk = y[:, p * LANES:(p + 1) * LANES]
            rot = pltpu.roll(blk, HEAD_DIM, 1)
            dst_ref[:, p * LANES:(p + 1) * LANES] = (
                blk * cos_ref[...] + rot * sin_ref[...]).astype(bf16)

    rope_store(0, cq_ref, sq_ref, q_ref)
    rope_store(1, ck_ref, sk_ref, k_ref)
    yv = jnp.dot(xn, w_ref[:, 2 * width:3 * width], preferred_element_type=f32)
    for t in range(tm // ATT_TILE):
        vt_ref[t] = yv[t * ATT_TILE:(t + 1) * ATT_TILE, :].T.astype(bf16)


def _norm_qkv(x2d, g, w, tables, seq):
    n, d = x2d.shape
    width = w.shape[1] // 3
    tm = ROW_TILE
    tiles_per_seq = seq // tm
    cq, sq, ck, sk = tables
    tab_spec = pl.BlockSpec((tm, LANES), lambda i: (i % tiles_per_seq, 0))
    return pl.pallas_call(
        functools.partial(_norm_qkv_body, width=width),
        grid=(n // tm,),
        in_specs=[
            pl.BlockSpec((tm, d), lambda i: (i, 0)),
            pl.BlockSpec((1, d), lambda i: (0, 0)),
            pl.BlockSpec((d, 3 * width), lambda i: (0, 0)),
            tab_spec, tab_spec, tab_spec, tab_spec,
        ],
        out_specs=[
            pl.BlockSpec((tm, width), lambda i: (i, 0)),
            pl.BlockSpec((tm, width), lambda i: (i, 0)),
            pl.BlockSpec((tm // ATT_TILE, width, ATT_TILE), lambda i: (i, 0, 0)),
        ],
        out_shape=[
            jax.ShapeDtypeStruct((n, width), bf16),
            jax.ShapeDtypeStruct((n, width), bf16),
            jax.ShapeDtypeStruct((n // ATT_TILE, width, ATT_TILE), bf16),
        ],
        compiler_params=_cparams(48, 1),
        name="norm_qkv_rope",
    )(x2d, g.reshape(1, d), w, cq, sq, ck, sk)


def _attn_body(*refs, mode, lambda_init, n_sub):
    if mode == "moba":
        (q_ref, k_ref, vt_ref, o_ref, qs_ref, sa_ref, sb_ref, m_ref, l_ref, acc_ref, kmean_ref,
         sel_ref) = refs
    elif mode == "dil":
        q_ref, k_ref, vt_ref, bias_ref, o_ref, qs_ref, sa_ref, sb_ref, m_ref, l_ref, acc_ref = refs
    else:
        (q_ref, k_ref, vt_ref, lam_ref, subln_ref, o_ref, qs_ref, sa_ref, sb_ref, m_ref, l_ref,
         acc_ref) = refs
    tq = tk = ATT_TILE
    ch = n_sub * tk
    i = pl.program_id(2)
    n_tiles = k_ref.shape[0] // tk
    dv = acc_ref.shape[1]
    last = i // n_sub

    lane = lax.broadcasted_iota(jnp.int32, (1, LANES), 1)
    elem = (lane % HEAD_DIM) // HALF
    qf = q_ref[...].astype(f32)
    for e in (0, 1):
        qs_ref[e] = jnp.where(elem == e, qf, 0.0).astype(bf16)
    m_ref[...] = jnp.full(m_ref.shape, NEG, f32)
    l_ref[...] = jnp.zeros(l_ref.shape, f32)
    acc_ref[...] = jnp.zeros(acc_ref.shape, f32)

    def scores(kt, e):
        return lax.dot_general(kt, qs_ref[e], (((1,), (1,)), ((), ())),
                               preferred_element_type=f32)

    def v_rows(vt, e):
        return vt if dv == LANES else vt[e * dv:(e + 1) * dv, :]

    if mode == "moba":
        @pl.when(i == 0)
        def _():
            for jb in range(n_tiles):
                blk = k_ref[jb * tk:(jb + 1) * tk, :].astype(f32)
                kmean_ref[jb:jb + 1, :] = jnp.sum(blk, axis=0, keepdims=True) * (1.0 / tk)

        km = kmean_ref[...]
        km_hi = km.astype(bf16)
        r1 = km - km_hi.astype(f32)
        km_mid = r1.astype(bf16)
        km_lo = (r1 - km_mid.astype(f32)).astype(bf16)
        rows = lax.broadcasted_iota(jnp.int32, (n_tiles, tq), 0)
        for e in (0, 1):
            gate = (scores(km_hi, e) + scores(km_mid, e)) + scores(km_lo, e)
            avail = rows < i
            sel = rows == i
            for _ in range(MOBA_TOPK):
                gm = jnp.where(avail, gate, -jnp.inf)
                mx = jnp.max(gm, axis=0, keepdims=True)
                cand = jnp.where(avail & (gm == mx), rows, n_tiles)
                pick = rows == jnp.min(cand, axis=0, keepdims=True)
                sel = sel | pick
                avail = avail & jnp.logical_not(pick)
            sel_ref[e] = jnp.where(sel, 1.0, 0.0)

    def qk(c, s_ref):
        kt = k_ref[pl.ds(pl.multiple_of(c * ch, ch), ch), :]
        for e in (0, 1):
            s_ref[e] = scores(kt, e)

    r_minus_c = (lax.broadcasted_iota(jnp.int32, (tk, tq), 0)
                 - lax.broadcasted_iota(jnp.int32, (tk, tq), 1))

    def process(c, s_ref, causal):
        for e in (0, 1):
            m_old = m_ref[e]
            m_new = m_old
            sels = []
            for t in range(n_sub):
                tile = c * n_sub + t
                rows_t = slice(t * tk, (t + 1) * tk)
                s = s_ref[e, rows_t, :]
                if mode == "dil":
                    s = s + bias_ref[i - tile + (n_sub - 1)]
                    s_ref[e, rows_t, :] = s
                elif causal:
                    s = s + jnp.where(r_minus_c <= (i - tile) * tk, 0.0, NEG)
                    s_ref[e, rows_t, :] = s
                cm = jnp.max(s, axis=0, keepdims=True)
                if mode == "moba":
                    sel = sel_ref[e, pl.ds(tile, 1), :] > 0.5
                    sels.append(sel)
                    cm = jnp.where(sel, cm, NEG)
                m_new = jnp.maximum(m_new, cm)
            alpha = jnp.exp2(m_old - m_new)
            l_new = alpha * l_ref[e]
            pv = None
            for t in range(n_sub):
                tile = c * n_sub + t
                s = s_ref[e, t * tk:(t + 1) * tk, :]
                mv = jnp.where(sels[t], m_new, -NEG) if mode == "moba" else m_new
                p = jnp.exp2(s - mv)
                l_new = l_new + jnp.sum(p, axis=0, keepdims=True)
                d = jnp.dot(v_rows(vt_ref[tile], e), p.astype(bf16), preferred_element_type=f32)
                pv = d if pv is None else pv + d
            m_ref[e] = m_new
            l_ref[e] = l_new
            acc_ref[e] = alpha * acc_ref[e] + pv

    if mode == "dil":
        n_back = bias_ref.shape[0] - 1 - 2 * (n_sub - 1)
        c0 = jnp.maximum(i - n_back, 0) // n_sub
    else:
        c0 = 0
    causal_last = mode != "dil"
    n_main = last - c0
    qk(c0, sa_ref)

    def pair_body(cc, carry):
        c = c0 + 2 * cc
        qk(c + 1, sb_ref)
        process(c, sa_ref, False)
        qk(c + 2, sa_ref)
        process(c + 1, sb_ref, False)
        return carry

    lax.fori_loop(0, n_main // 2, pair_body, 0)

    @pl.when(n_main % 2 == 1)
    def _():
        qk(last, sb_ref)
        process(last - 1, sa_ref, False)
        process(last, sb_ref, causal_last)

    @pl.when(n_main % 2 == 0)
    def _():
        process(last, sa_ref, causal_last)

    outs = [acc_ref[e] * (1.0 / l_ref[e]) for e in (0, 1)]
    if mode == "diff":
        lv = lam_ref[...]
        lam = (jnp.exp(jnp.sum(lv[0:1] * lv[1:2], axis=-1, keepdims=True))
               - jnp.exp(jnp.sum(lv[2:3] * lv[3:4], axis=-1, keepdims=True)) + lambda_init)
        o_t = outs[0] - lam * outs[1]
        ms = jnp.mean(o_t * o_t, axis=0, keepdims=True)
        o_t = (o_t * lax.rsqrt(ms + DIFF_EPS) * subln_ref[...]) * (1.0 - lambda_init)
    else:
        o_t = jnp.concatenate(outs, axis=0)
    o_ref[...] = o_t.T.astype(bf16)


def _attention(q, k, vt, *, mode, pair_off, n_pairs, batch, seq, extra=(), lambda_init=0.0):
    n, width = q.shape
    tq = ATT_TILE
    n_tiles = seq // tq
    n_sub = KV_SUB
    assert n_tiles % n_sub == 0
    dv = LANES if mode == "diff" else HEAD_DIM
    in_specs = [
        pl.BlockSpec((tq, LANES), lambda b, p, i: (b * n_tiles + i, pair_off + p)),
        pl.BlockSpec((seq, LANES), lambda b, p, i: (b, pair_off + p)),
        pl.BlockSpec((n_tiles, LANES, tq), lambda b, p, i: (b, pair_off + p, 0)),
    ]
    scratch = [
        pltpu.VMEM((2, tq, LANES), bf16),
        pltpu.VMEM((2, n_sub * tq, tq), f32),
        pltpu.VMEM((2, n_sub * tq, tq), f32),
        pltpu.VMEM((2, 1, tq), f32),
        pltpu.VMEM((2, 1, tq), f32),
        pltpu.VMEM((2, dv, tq), f32),
    ]
    if mode == "moba":
        scratch += [pltpu.VMEM((n_tiles, LANES), f32), pltpu.VMEM((2, n_tiles, tq), f32)]
    elif mode == "dil":
        (bias,) = extra
        in_specs.append(pl.BlockSpec(bias.shape, lambda b, p, i: (0, 0, 0)))
    else:
        lam_vecs, subln_b = extra
        in_specs.append(pl.BlockSpec(lam_vecs.shape, lambda b, p, i: (0, 0)))
        in_specs.append(pl.BlockSpec(subln_b.shape, lambda b, p, i: (0, 0)))
    return pl.pallas_call(
        functools.partial(_attn_body, mode=mode, lambda_init=lambda_init, n_sub=n_sub),
        grid=(batch, n_pairs, n_tiles),
        in_specs=in_specs,
        out_specs=pl.BlockSpec((tq, LANES), lambda b, p, i: (b * n_tiles + i, p)),
        out_shape=jax.ShapeDtypeStruct((n, n_pairs * LANES), bf16),
        scratch_shapes=scratch,
        compiler_params=_cparams(32, 3),
        name="attn_" + mode,
    )(q, k, vt, *extra)


def _mlp_body(*refs, n_o, final, tiles_per_seq):
    x_ref = refs[0]
    o_refs = refs[1:1 + n_o]
    wo_refs = refs[1 + n_o:1 + 2 * n_o]
    g_ref, wup_ref, cw_ref, cb_ref, wdn_ref = refs[1 + 2 * n_o:6 + 2 * n_o]
    rest = refs[6 + 2 * n_o:]
    if final:
        fg_ref, out_ref, ubuf_ref, abuf_ref, carry_ref = rest
    else:
        out_ref, ubuf_ref, abuf_ref, carry_ref = rest
    tm = x_ref.shape[0]
    i = pl.program_id(0)

    x1 = x_ref[...]
    for o_ref, wo_ref in zip(o_refs, wo_refs):
        x1 = x1 + jnp.dot(o_ref[...], wo_ref[...], preferred_element_type=f32)
    out_ref[...] = x1
    xn = _rms(x1, g_ref[...], NORM_EPS).astype(bf16)

    @pl.when(i % tiles_per_seq == 0)
    def _():
        carry_ref[...] = jnp.zeros_like(carry_ref)

    def conv(slot, col0):
        cols = slice(col0, col0 + FF_CHUNK)
        u = jnp.dot(xn, wup_ref[:, cols], preferred_element_type=f32)
        ubuf_ref[slot, 0:SUBLANES, :] = carry_ref[:, cols]
        ubuf_ref[slot, SUBLANES:SUBLANES + tm, :] = u
        carry_ref[:, cols] = u[tm - SUBLANES:tm, :]
        u1 = ubuf_ref[slot, pl.ds(SUBLANES - 1, tm), :]
        u2 = ubuf_ref[slot, pl.ds(SUBLANES - 2, tm), :]
        w = cw_ref[:, cols]
        return (u * w[2:3] + u1 * w[1:2]) + (u2 * w[0:1] + cb_ref[:, cols])

    for c in range(D_FF // FF_CHUNK):
        slot = 2 * (c % 2)
        gate = conv(slot, c * FF_CHUNK)
        val = conv(slot + 1, D_FF + c * FF_CHUNK)
        act = gate * (1.0 / (1.0 + jnp.exp(-gate))) * val
        abuf_ref[:, c * FF_CHUNK:(c + 1) * FF_CHUNK] = act.astype(bf16)

    x2 = out_ref[...] + jnp.dot(abuf_ref[...], wdn_ref[...], preferred_element_type=f32)
    if final:
        x2 = _rms(x2, fg_ref[...], NORM_EPS)
    out_ref[...] = x2


def _resident(shape):
    return pl.BlockSpec(shape, lambda i: (0,) * len(shape), pipeline_mode=pl.Buffered(1))


def _mlp(x2d, o_list, wo_list, g, w_up, conv_w, conv_b, w_down, final_g, seq):
    n, d = x2d.shape
    tm = ROW_TILE
    n_o = len(o_list)
    final = final_g is not None
    in_specs = [pl.BlockSpec((tm, d), lambda i: (i, 0))]
    in_specs += [pl.BlockSpec((tm, o.shape[1]), lambda i: (i, 0)) for o in o_list]
    in_specs += [_resident(w.shape) for w in wo_list]
    in_specs += [_resident((1, d)), _resident(w_up.shape), _resident(conv_w.shape),
                 _resident((1, 2 * D_FF)), _resident(w_down.shape)]
    args = [x2d, *o_list, *wo_list, g.reshape(1, d), w_up, conv_w, conv_b.reshape(1, 2 * D_FF),
            w_down]
    if final:
        in_specs.append(_resident((1, d)))
        args.append(final_g.reshape(1, d))
    return pl.pallas_call(
        functools.partial(_mlp_body, n_o=n_o, final=final, tiles_per_seq=seq // tm),
        grid=(n // tm,),
        in_specs=in_specs,
        out_specs=pl.BlockSpec((tm, d), lambda i: (i, 0)),
        out_shape=jax.ShapeDtypeStruct((n, d), f32),
        scratch_shapes=[
            pltpu.VMEM((4, tm + SUBLANES, FF_CHUNK), f32),
            pltpu.VMEM((tm, D_FF), bf16),
            pltpu.VMEM((SUBLANES, 2 * D_FF), f32),
        ],
        compiler_params=_cparams(56, 1),
        name="outproj_convmlp",
    )(*args)


def kernel(x, even_norm, even_w_in, even_w_out, odd_norm, odd_w_qkv, odd_lambda_q1, odd_lambda_k1,
           odd_lambda_q2, odd_lambda_k2, odd_subln, odd_w_out, ffn_norm, ffn_w_up, ffn_conv_w,
           ffn_conv_b, ffn_w_down, final_norm):
    batch, seq, d = x.shape
    depth = ffn_norm.shape[0]
    assert seq % ROW_TILE == 0 and ROW_TILE % ATT_TILE == 0
    x2d = x.reshape(batch * seq, d)

    cos_t, sin_t = _rope_tables(seq)
    scale = HEAD_DIM ** -0.5 * LOG2E
    tables = (cos_t * scale, sin_t * scale, cos_t, sin_t)
    dil_bias = _dilated_bias_tables(KV_SUB)

    for layer in range(depth):
        li = layer // 2
        if layer % 2 == 0:
            w_in = _prep_qkv_weight(even_w_in[li])
            q, k, vt = _norm_qkv(x2d, even_norm[li], w_in, tables, seq)
            n_pairs = q.shape[1] // LANES
            n_moba = n_pairs // 2
            common = dict(batch=batch, seq=seq)
            o_a = _attention(q, k, vt, mode="moba", pair_off=0, n_pairs=n_moba, **common)
            o_b = _attention(q, k, vt, mode="dil", pair_off=n_moba, n_pairs=n_pairs - n_moba,
                             extra=(dil_bias,), **common)
            w_out = even_w_out[li].astype(bf16)
            split = n_moba * LANES
            o_list, wo_list = [o_a, o_b], [w_out[:split], w_out[split:]]
        else:
            lambda_init = 0.8 - 0.6 * math.exp(-0.3 * layer)
            w_qkv = _prep_qkv_weight(odd_w_qkv[li])
            q, k, vt = _norm_qkv(x2d, odd_norm[li], w_qkv, tables, seq)
            lam_vecs = jnp.stack([odd_lambda_q1[li], odd_lambda_k1[li],
                                  odd_lambda_q2[li], odd_lambda_k2[li]]).astype(f32)
            subln_b = jnp.broadcast_to(odd_subln[li].astype(f32)[:, None], (LANES, ATT_TILE))
            o = _attention(q, k, vt, mode="diff", pair_off=0, n_pairs=q.shape[1] // LANES,
                           batch=batch, seq=seq, extra=(lam_vecs, subln_b),
                           lambda_init=lambda_init)
            o_list, wo_list = [o], [odd_w_out[li].astype(bf16)]
        final_g = final_norm if layer == depth - 1 else None
        x2d = _mlp(x2d, o_list, wo_list, ffn_norm[layer], ffn_w_up[layer].astype(bf16),
                   ffn_conv_w[layer], ffn_conv_b[layer], ffn_w_down[layer].astype(bf16),
                   final_g, seq)
    return x2d.reshape(batch, seq, d)
```

```python
import functools
import math

import numpy as np
import jax
import jax.numpy as jnp
from jax import lax
from jax.experimental import pallas as pl
from jax.experimental.pallas import tpu as pltpu

f32 = jnp.float32
bf16 = jnp.bfloat16

HEAD_DIM = 64
HALF = HEAD_DIM // 2
MOBA_BLOCK = 256
MOBA_TOPK = 3
DIL_PAIRS = ((128, 1), (512, 4), (2048, 16))
DIFF_EPS = 1e-5
NORM_EPS = 1e-6
ROPE_THETA = 10000.0
D_FF = 2816
CONV_WIDTH = 3

LANES = 128
ATT_TILE = 256
ROW_TILE = 512
FF_CHUNK = 256
KV_SUB = 2
PIPE_UNROLL = 4
TILE_UNROLL = 4
LOG2E = math.log2(math.e)
SUBLANES = 8
NEG = -1e30

assert ATT_TILE == MOBA_BLOCK


def _cparams(vmem_mb, n_axes):
    return pltpu.CompilerParams(
        dimension_semantics=("arbitrary",) * n_axes,
        vmem_limit_bytes=vmem_mb * 1024 * 1024)


def _pair_perm(width):
    idx = np.arange(width)
    p, n = idx // LANES, idx % LANES
    hi, e, d = n // HEAD_DIM, (n % HEAD_DIM) // HALF, n % HALF
    return p * LANES + e * HEAD_DIM + hi * HALF + d


def _prep_qkv_weight(w):
    width = w.shape[1] // 3
    perm = _pair_perm(width)
    cols = np.concatenate([perm, width + perm, 2 * width + np.arange(width)])
    return jnp.take(w, jnp.asarray(cols, dtype=jnp.int32), axis=1).astype(bf16)


def _rope_tables(seq):
    inv = 1.0 / (ROPE_THETA ** (jnp.arange(HALF, dtype=f32) * 2.0 / HEAD_DIM))
    ang = jnp.arange(seq, dtype=f32)[:, None] * inv[None, :]
    cos, sin = jnp.cos(ang), jnp.sin(ang)
    cos_t = jnp.concatenate([cos, cos, cos, cos], axis=1)
    sin_t = jnp.concatenate([-sin, -sin, sin, sin], axis=1)
    return cos_t, sin_t


def _dilated_bias_tables(n_sub):
    max_w = max(w for w, _ in DIL_PAIRS)
    n_back = max_w // ATT_TILE
    r = np.arange(ATT_TILE)[:, None]
    c = np.arange(ATT_TILE)[None, :]
    tabs = []
    for back in range(-(n_sub - 1), n_back + n_sub):
        delta = back * ATT_TILE + c - r
        cnt = np.zeros_like(delta)
        for w, dil in DIL_PAIRS:
            cnt += ((delta >= 0) & (delta <= w) & (delta % dil == 0)).astype(cnt.dtype)
        with np.errstate(divide="ignore"):
            tabs.append(np.where(cnt > 0, np.log2(np.maximum(cnt, 1).astype(np.float64)), NEG))
    return jnp.asarray(np.stack(tabs), dtype=f32)


def _rms(x, g, eps):
    ms = jnp.mean(x * x, axis=-1, keepdims=True)
    return x * lax.rsqrt(ms + eps) * g


def _norm_qkv_body(x_ref, g_ref, w_ref, cq_ref, sq_ref, ck_ref, sk_ref, q_ref, k_ref, vt_ref, *,
                   width):
    xn = _rms(x_ref[...], g_ref[...], NORM_EPS).astype(bf16)
    tm = x_ref.shape[0]

    def rope_store(part, cos_ref, sin_ref, dst_ref):
        y = jnp.dot(xn, w_ref[:, part * width:(part + 1) * width], preferred_element_type=f32)
        for p in range(width // LANES):
            blk = y[:, p * LANES:(p + 1) * LANES]
            rot = pltpu.roll(blk, HEAD_DIM, 1)
            dst_ref[:, p * LANES:(p + 1) * LANES] = (
                blk * cos_ref[...] + rot * sin_ref[...]).astype(bf16)

    rope_store(0, cq_ref, sq_ref, q_ref)
    rope_store(1, ck_ref, sk_ref, k_ref)
    yv = jnp.dot(xn, w_ref[:, 2 * width:3 * width], preferred_element_type=f32)
    for t in range(tm // ATT_TILE):
        vt_ref[t] = yv[t * ATT_TILE:(t + 1) * ATT_TILE, :].T.astype(bf16)


def _norm_qkv(x2d, g, w, tables, seq):
    n, d = x2d.shape
    width = w.shape[1] // 3
    tm = ROW_TILE
    tiles_per_seq = seq // tm
    cq, sq, ck, sk = tables
    tab_spec = pl.BlockSpec((tm, LANES), lambda i: (i % tiles_per_seq, 0))
    return pl.pallas_call(
        functools.partial(_norm_qkv_body, width=width),
        grid=(n // tm,),
        in_specs=[
            pl.BlockSpec((tm, d), lambda i: (i, 0)),
            pl.BlockSpec((1, d), lambda i: (0, 0)),
            pl.BlockSpec((d, 3 * width), lambda i: (0, 0)),
            tab_spec, tab_spec, tab_spec, tab_spec,
        ],
        out_specs=[
            pl.BlockSpec((tm, width), lambda i: (i, 0)),
            pl.BlockSpec((tm, width), lambda i: (i, 0)),
            pl.BlockSpec((tm // ATT_TILE, width, ATT_TILE), lambda i: (i, 0, 0)),
        ],
        out_shape=[
            jax.ShapeDtypeStruct((n, width), bf16),
            jax.ShapeDtypeStruct((n, width), bf16),
            jax.ShapeDtypeStruct((n // ATT_TILE, width, ATT_TILE), bf16),
        ],
        compiler_params=_cparams(48, 1),
        name="norm_qkv_rope",
    )(x2d, g.reshape(1, d), w, cq, sq, ck, sk)


def _item_tables(mode, n_tiles, n_sub, n_back):
    if mode == "dil":
        phases = [[(i, c) for i in range(n_tiles)
                   for c in range(max(i - n_back, 0) // n_sub, i // n_sub + 1)]]
    else:
        phases = [[(i, c) for i in range(n_tiles) for c in range(i // n_sub)],
                  [(i, i // n_sub) for i in range(n_tiles)]]
    for ph in phases:
        assert len(ph) % PIPE_UNROLL == 0, "phase length must fill whole pipeline bodies"
    flat = np.asarray([v for ph in phases for it in ph for v in it], dtype=np.int32)
    return flat, [len(ph) for ph in phases]


def _attn_body(tab_ref, *refs, mode, lambda_init, n_sub, phase_sizes):
    q_ref, k_ref, vt_ref = refs[:3]
    n_extra = {"moba": 0, "dil": 1, "diff": 2}[mode]
    if mode == "dil":
        (bias_ref,) = refs[3:4]
    elif mode == "diff":
        lam_ref, subln_ref = refs[3:5]
    o_ref, qs_ref = refs[3 + n_extra:5 + n_extra]
    n_fixed = 5 + n_extra
    s_bufs = refs[n_fixed:n_fixed + PIPE_UNROLL]
    p_bufs = refs[n_fixed + PIPE_UNROLL:n_fixed + 2 * PIPE_UNROLL]
    (cm_ref, al_ref, m_ref, acc_ref,
     vx_ref) = refs[n_fixed + 2 * PIPE_UNROLL:n_fixed + 2 * PIPE_UNROLL + 5]
    if mode == "moba":
        kstack_ref, sel_ref = refs[n_fixed + 2 * PIPE_UNROLL + 5:]
    tq = tk = ATT_TILE
    ch = n_sub * tk
    n_tiles = k_ref.shape[0] // tk
    dvx = acc_ref.shape[2]
    dv = dvx - 2 * SUBLANES
    n_v = vx_ref.shape[1]

    def tile_rows(i):
        return pl.ds(pl.multiple_of(i * tq, tq), tq)

    lane = lax.broadcasted_iota(jnp.int32, (1, LANES), 1)
    elem = (lane % HEAD_DIM) // HALF
    ones_rows = jnp.ones((2 * SUBLANES, tk), bf16)

    def setup_tile(i, carry):
        qf = q_ref[tile_rows(i), :].astype(f32)
        for e in (0, 1):
            qs_ref[e, i] = jnp.where(elem == e, qf, 0.0).astype(bf16)
        vt = vt_ref[i]
        for e in range(n_v):
            vx_ref[i, e, 0:dv, :] = vt if n_v == 1 else vt[e * dv:(e + 1) * dv, :]
            vx_ref[i, e, dv:dvx, :] = ones_rows
        return carry

    lax.fori_loop(0, n_tiles, setup_tile, 0)
    m_ref[...] = jnp.full(m_ref.shape, NEG, f32)
    acc_ref[...] = jnp.zeros(acc_ref.shape, f32)
    al_ref[...] = jnp.ones(al_ref.shape, f32)
    p_bufs[PIPE_UNROLL - 1][...] = jnp.zeros(p_bufs[PIPE_UNROLL - 1].shape, bf16)

    def scores(kt, e, i):
        return lax.dot_general(kt, qs_ref[e, i], (((1,), (1,)), ((), ())),
                               preferred_element_type=f32)

    if mode == "moba":
        nb = n_tiles
        for jb in range(nb):
            blk = k_ref[jb * tk:(jb + 1) * tk, :].astype(f32)
            km = jnp.sum(blk, axis=0, keepdims=True) * (1.0 / tk)
            km_hi = km.astype(bf16)
            r1 = km - km_hi.astype(f32)
            km_mid = r1.astype(bf16)
            km_lo = (r1 - km_mid.astype(f32)).astype(bf16)
            kstack_ref[jb:jb + 1, :] = km_hi
            kstack_ref[nb + jb:nb + jb + 1, :] = km_mid
            kstack_ref[2 * nb + jb:2 * nb + jb + 1, :] = km_lo

        rows = lax.broadcasted_iota(jnp.int32, (nb, tq), 0)

        def select_tiles(g, carry):
            for u in range(PIPE_UNROLL):
                i = g * PIPE_UNROLL + u
                for e in (0, 1):
                    g3 = scores(kstack_ref[...], e, i)
                    gate = (g3[0:nb] + g3[nb:2 * nb]) + g3[2 * nb:3 * nb]
                    avail = rows < i
                    sel = rows == i
                    for _ in range(MOBA_TOPK):
                        gm = jnp.where(avail, gate, -jnp.inf)
                        mx = jnp.max(gm, axis=0, keepdims=True)
                        cand = jnp.where(avail & (gm == mx), rows, nb)
                        pick = rows == jnp.min(cand, axis=0, keepdims=True)
                        sel = sel | pick
                        avail = avail & jnp.logical_not(pick)
                    sel_ref[i, e] = jnp.where(sel, 1.0, 0.0)
            return carry

        lax.fori_loop(0, n_tiles // PIPE_UNROLL, select_tiles, 0)

    def item(w):
        return tab_ref[2 * w], tab_ref[2 * w + 1]

    r_minus_c = (lax.broadcasted_iota(jnp.int32, (tk, tq), 0)
                 - lax.broadcasted_iota(jnp.int32, (tk, tq), 1))

    def qk(it, buf, causal):
        i, c = it
        kt = k_ref[pl.ds(pl.multiple_of(c * ch, ch), ch), :]
        for e in (0, 1):
            s = scores(kt, e, i)
            for t in range(n_sub):
                tile = c * n_sub + t
                st = s[t * tk:(t + 1) * tk]
                if mode == "dil":
                    st = st + bias_ref[i - tile + (n_sub - 1)]
                elif causal:
                    st = st + jnp.where(r_minus_c <= (i - tile) * tk, 0.0, NEG)
                s_bufs[buf][e, t * tk:(t + 1) * tk, :] = st
                cm_ref[buf, e, t] = jnp.max(st, axis=0, keepdims=True)

    def softmax(it, buf):
        p_ref = p_bufs[buf]
        i, c = it
        for e in (0, 1):
            m_old = m_ref[i, e]
            m_new = m_old
            sels = []
            for t in range(n_sub):
                cm = cm_ref[buf, e, t]
                if mode == "moba":
                    sel = sel_ref[i, e, pl.ds(c * n_sub + t, 1), :] > 0.5
                    sels.append(sel)
                    cm = jnp.where(sel, cm, NEG)
                m_new = jnp.maximum(m_new, cm)
            al_ref[buf, e] = jnp.exp2(m_old - m_new)
            m_ref[i, e] = m_new
            for t in range(n_sub):
                rows_t = slice(t * tk, (t + 1) * tk)
                mv = jnp.where(sels[t], m_new, -NEG) if mode == "moba" else m_new
                p_ref[e, rows_t, :] = jnp.exp2(s_bufs[buf][e, rows_t, :] - mv).astype(bf16)

    def pv_apply(it, buf):
        i, c = it
        p_ref = p_bufs[buf]
        for e in (0, 1):
            pv = None
            for t in range(n_sub):
                d = jnp.dot(vx_ref[c * n_sub + t, e % n_v], p_ref[e, t * tk:(t + 1) * tk, :],
                            preferred_element_type=f32)
                pv = d if pv is None else pv + d
            acc_ref[i, e] = al_ref[buf, e] * acc_ref[i, e] + pv

    last_p = PIPE_UNROLL - 1

    def run_phase(start, n_items, causal, pend):
        end = start + n_items - 1
        qk(item(start), 0, causal)
        qk(item(start + 1), 1, causal)

        def body(h, pend):
            w = start + PIPE_UNROLL * h
            its = [item(jnp.minimum(w + j, end)) for j in range(PIPE_UNROLL + 2)]
            for j in range(PIPE_UNROLL):
                if j % 2 == 0:
                    qk(its[j + 2], (j + 2) % PIPE_UNROLL, causal)
                    qk(its[j + 3], (j + 3) % PIPE_UNROLL, causal)
                pv_apply(pend, (j + last_p) % PIPE_UNROLL)
                softmax(its[j], j)
                pend = its[j]
            return pend

        return lax.fori_loop(0, n_items // PIPE_UNROLL, body, pend)

    pend = (jnp.int32(0), jnp.int32(0))
    start = 0
    for ph, n_items in enumerate(phase_sizes):
        pend = run_phase(start, n_items, mode != "dil" and ph == len(phase_sizes) - 1, pend)
        start += n_items
    pv_apply(pend, last_p)

    if mode == "diff":
        lv = lam_ref[...]
        lam = (jnp.exp(jnp.sum(lv[0:1] * lv[1:2], axis=-1, keepdims=True))
               - jnp.exp(jnp.sum(lv[2:3] * lv[3:4], axis=-1, keepdims=True)) + lambda_init)

    def finish_tile(i, carry):
        outs = []
        for e in (0, 1):
            a = acc_ref[i, e]
            outs.append(a[0:dv] * (1.0 / a[dv:dv + 1]))
        if mode == "diff":
            o_t = outs[0] - lam * outs[1]
            ms = jnp.mean(o_t * o_t, axis=0, keepdims=True)
            o_t = (o_t * lax.rsqrt(ms + DIFF_EPS) * subln_ref[...]) * (1.0 - lambda_init)
        else:
            o_t = jnp.concatenate(outs, axis=0)
        o_ref[tile_rows(i), :] = o_t.T.astype(bf16)
        return carry

    lax.fori_loop(0, n_tiles, finish_tile, 0, unroll=TILE_UNROLL)


def _attention(q, k, vt, *, mode, pair_off, n_pairs, batch, seq, extra=(), lambda_init=0.0):
    n, width = q.shape
    tq = ATT_TILE
    n_tiles = seq // tq
    n_sub = KV_SUB
    assert n_tiles % n_sub == 0 and n_tiles % PIPE_UNROLL == 0
    dv = LANES if mode == "diff" else HEAD_DIM
    dvx = dv + 2 * SUBLANES
    n_v = 1 if mode == "diff" else 2
    n_back = max(w for w, _ in DIL_PAIRS) // ATT_TILE
    table, phase_sizes = _item_tables(mode, n_tiles, n_sub, n_back)
    in_specs = [
        pl.BlockSpec((seq, LANES), lambda b, p, tab: (b, pair_off + p)),
        pl.BlockSpec((seq, LANES), lambda b, p, tab: (b, pair_off + p)),
        pl.BlockSpec((n_tiles, LANES, tq), lambda b, p, tab: (b, pair_off + p, 0)),
    ]
    scratch = [
        pltpu.VMEM((2, n_tiles, tq, LANES), bf16),
        *[pltpu.VMEM((2, n_sub * tq, tq), f32)
          for _ in range(PIPE_UNROLL)],
        *[pltpu.VMEM((2, n_sub * tq, tq), bf16)
          for _ in range(PIPE_UNROLL)],
        pltpu.VMEM((PIPE_UNROLL, 2, n_sub, 1, tq), f32),
        pltpu.VMEM((PIPE_UNROLL, 2, 1, tq), f32),
        pltpu.VMEM((n_tiles, 2, 1, tq), f32),
        pltpu.VMEM((n_tiles, 2, dvx, tq), f32),
        pltpu.VMEM((n_tiles, n_v, dvx, tq), bf16),
    ]
    if mode == "moba":
        scratch += [pltpu.VMEM((MOBA_TOPK * n_tiles, LANES), bf16),
                    pltpu.VMEM((n_tiles, 2, n_tiles, tq), f32)]
    elif mode == "dil":
        (bias,) = extra
        in_specs.append(pl.BlockSpec(bias.shape, lambda b, p, tab: (0, 0, 0)))
    else:
        lam_vecs, subln_b = extra
        in_specs.append(pl.BlockSpec(lam_vecs.shape, lambda b, p, tab: (0, 0)))
        in_specs.append(pl.BlockSpec(subln_b.shape, lambda b, p, tab: (0, 0)))
    return pl.pallas_call(
        functools.partial(_attn_body, mode=mode, lambda_init=lambda_init, n_sub=n_sub,
                          phase_sizes=tuple(phase_sizes)),
        grid_spec=pltpu.PrefetchScalarGridSpec(
            num_scalar_prefetch=1,
            grid=(batch, n_pairs),
            in_specs=in_specs,
            out_specs=pl.BlockSpec((seq, LANES), lambda b, p, tab: (b, p)),
            scratch_shapes=scratch),
        out_shape=jax.ShapeDtypeStruct((n, n_pairs * LANES), bf16),
        compiler_params=_cparams(40, 2),
        name="attn_" + mode,
    )(jnp.asarray(table), q, k, vt, *extra)


def _mlp_body(*refs, n_o, final, tiles_per_seq):
    x_ref = refs[0]
    o_refs = refs[1:1 + n_o]
    wo_refs = refs[1 + n_o:1 + 2 * n_o]
    g_ref, wup_ref, cw_ref, cb_ref, wdn_ref = refs[1 + 2 * n_o:6 + 2 * n_o]
    rest = refs[6 + 2 * n_o:]
    if final:
        fg_ref, out_ref, ubuf_ref, abuf_ref, carry_ref = rest
    else:
        out_ref, ubuf_ref, abuf_ref, carry_ref = rest
    tm = x_ref.shape[0]
    i = pl.program_id(0)

    x1 = x_ref[...]
    for o_ref, wo_ref in zip(o_refs, wo_refs):
        x1 = x1 + jnp.dot(o_ref[...], wo_ref[...], preferred_element_type=f32)
    out_ref[...] = x1
    xn = _rms(x1, g_ref[...], NORM_EPS).astype(bf16)

    @pl.when(i % tiles_per_seq == 0)
    def _():
        carry_ref[...] = jnp.zeros_like(carry_ref)

    def conv(slot, col0):
        cols = slice(col0, col0 + FF_CHUNK)
        u = jnp.dot(xn, wup_ref[:, cols], preferred_element_type=f32)
        ubuf_ref[slot, 0:SUBLANES, :] = carry_ref[:, cols]
        ubuf_ref[slot, SUBLANES:SUBLANES + tm, :] = u
        carry_ref[:, cols] = u[tm - SUBLANES:tm, :]
        u1 = ubuf_ref[slot, pl.ds(SUBLANES - 1, tm), :]
        u2 = ubuf_ref[slot, pl.ds(SUBLANES - 2, tm), :]
        w = cw_ref[:, cols]
        return (u * w[2:3] + u1 * w[1:2]) + (u2 * w[0:1] + cb_ref[:, cols])

    for c in range(D_FF // FF_CHUNK):
        slot = 2 * (c % 2)
        gate = conv(slot, c * FF_CHUNK)
        val = conv(slot + 1, D_FF + c * FF_CHUNK)
        act = gate * (1.0 / (1.0 + jnp.exp(-gate))) * val
        abuf_ref[:, c * FF_CHUNK:(c + 1) * FF_CHUNK] = act.astype(bf16)

    x2 = out_ref[...] + jnp.dot(abuf_ref[...], wdn_ref[...], preferred_element_type=f32)
    if final:
        x2 = _rms(x2, fg_ref[...], NORM_EPS)
    out_ref[...] = x2


def _resident(shape):
    return pl.BlockSpec(shape, lambda i: (0,) * len(shape), pipeline_mode=pl.Buffered(1))


def _mlp(x2d, o_list, wo_list, g, w_up, conv_w, conv_b, w_down, final_g, seq):
    n, d = x2d.shape
    tm = ROW_TILE
    n_o = len(o_list)
    final = final_g is not None
    in_specs = [pl.BlockSpec((tm, d), lambda i: (i, 0))]
    in_specs += [pl.BlockSpec((tm, o.shape[1]), lambda i: (i, 0)) for o in o_list]
    in_specs += [_resident(w.shape) for w in wo_list]
    in_specs += [_resident((1, d)), _resident(w_up.shape), _resident(conv_w.shape),
                 _resident((1, 2 * D_FF)), _resident(w_down.shape)]
    args = [x2d, *o_list, *wo_list, g.reshape(1, d), w_up, conv_w, conv_b.reshape(1, 2 * D_FF),
            w_down]
    if final:
        in_specs.append(_resident((1, d)))
        args.append(final_g.reshape(1, d))
    return pl.pallas_call(
        functools.partial(_mlp_body, n_o=n_o, final=final, tiles_per_seq=seq // tm),
        grid=(n // tm,),
        in_specs=in_specs,
        out_specs=pl.BlockSpec((tm, d), lambda i: (i, 0)),
        out_shape=jax.ShapeDtypeStruct((n, d), f32),
        scratch_shapes=[
            pltpu.VMEM((4, tm + SUBLANES, FF_CHUNK), f32),
            pltpu.VMEM((tm, D_FF), bf16),
            pltpu.VMEM((SUBLANES, 2 * D_FF), f32),
        ],
        compiler_params=_cparams(56, 1),
        name="outproj_convmlp",
    )(*args)


def kernel(x, even_norm, even_w_in, even_w_out, odd_norm, odd_w_qkv, odd_lambda_q1, odd_lambda_k1,
           odd_lambda_q2, odd_lambda_k2, odd_subln, odd_w_out, ffn_norm, ffn_w_up, ffn_conv_w,
           ffn_conv_b, ffn_w_down, final_norm):
    batch, seq, d = x.shape
    depth = ffn_norm.shape[0]
    assert seq % ROW_TILE == 0 and ROW_TILE % ATT_TILE == 0
    x2d = x.reshape(batch * seq, d)

    cos_t, sin_t = _rope_tables(seq)
    scale = HEAD_DIM ** -0.5 * LOG2E
    tables = (cos_t * scale, sin_t * scale, cos_t, sin_t)
    dil_bias = _dilated_bias_tables(KV_SUB)

    for layer in range(depth):
        li = layer // 2
        if layer % 2 == 0:
            w_in = _prep_qkv_weight(even_w_in[li])
            q, k, vt = _norm_qkv(x2d, even_norm[li], w_in, tables, seq)
            n_pairs = q.shape[1] // LANES
            n_moba = n_pairs // 2
            common = dict(batch=batch, seq=seq)
            o_a = _attention(q, k, vt, mode="moba", pair_off=0, n_pairs=n_moba, **common)
            o_b = _attention(q, k, vt, mode="dil", pair_off=n_moba, n_pairs=n_pairs - n_moba,
                             extra=(dil_bias,), **common)
            w_out = even_w_out[li].astype(bf16)
            split = n_moba * LANES
            o_list, wo_list = [o_a, o_b], [w_out[:split], w_out[split:]]
        else:
            lambda_init = 0.8 - 0.6 * math.exp(-0.3 * layer)
            w_qkv = _prep_qkv_weight(odd_w_qkv[li])
            q, k, vt = _norm_qkv(x2d, odd_norm[li], w_qkv, tables, seq)
            lam_vecs = jnp.stack([odd_lambda_q1[li], odd_lambda_k1[li],
                                  odd_lambda_q2[li], odd_lambda_k2[li]]).astype(f32)
            subln_b = jnp.broadcast_to(odd_subln[li].astype(f32)[:, None], (LANES, ATT_TILE))
            o = _attention(q, k, vt, mode="diff", pair_off=0, n_pairs=q.shape[1] // LANES,
                           batch=batch, seq=seq, extra=(lam_vecs, subln_b),
                           lambda_init=lambda_init)
            o_list, wo_list = [o], [odd_w_out[li].astype(bf16)]
        final_g = final_norm if layer == depth - 1 else None
        x2d = _mlp(x2d, o_list, wo_list, ffn_norm[layer], ffn_w_up[layer].astype(bf16),
                   ffn_conv_w[layer], ffn_conv_b[layer], ffn_w_down[layer].astype(bf16),
                   final_g, seq)
    return x2d.reshape(batch, seq, d)
```
